```python
import math
import jax, jax.numpy as jnp
from jax import lax
import numpy as np

D_MODEL = 2048
BATCH = 4
SEQ = 2048
DEPTH = 2
DEC_BATCH = 128
DEC_SEQ = 1
PAST_LEN = 16384
PAGE_SIZE = 128

A_HEADS = 4
A_DK = 128
A_DV = 128
A_WIDTH = A_HEADS * A_DV
B_HEADS = 8
B_DK = 128
B_DV = 128
B_WIDTH = B_HEADS * B_DV
C_BLOCKS = 8
C_WIDTH = 512
C_BLOCK = C_WIDTH // C_BLOCKS
RG_C = 8.0
CONV_W = 4
MIX_WIDTH = A_WIDTH + B_WIDTH + C_WIDTH
D_FF = -(-8 * D_MODEL // (3 * 256)) * 256
CHUNK = 64
NORM_EPS = 1e-6
L2_EPS = 1e-6
IN_SIZES = (A_HEADS * A_DK, A_HEADS * A_DK, A_WIDTH, A_WIDTH,
            B_HEADS * B_DK, B_HEADS * B_DK, B_WIDTH, B_WIDTH,
            B_HEADS, B_HEADS,
            C_WIDTH, C_WIDTH)
IN_TOTAL = sum(IN_SIZES)

kernel_name = "hybrid_hgrn2_gdn_rglru_decode_step"


def _split_points():
    pts, acc = [], 0
    for s in IN_SIZES[:-1]:
        acc += s
        pts.append(acc)
    return pts


def rmsnorm(x, g):
    xf = x.astype(jnp.float32)
    return xf * lax.rsqrt(jnp.mean(xf * xf, axis=-1, keepdims=True) + NORM_EPS) * g.astype(jnp.float32)


def l2norm(x):
    return x * lax.rsqrt(jnp.sum(x * x, axis=-1, keepdims=True) + L2_EPS)


def causal_dwconv(x, buf, w, b):
    L = x.shape[1]
    xc = jnp.concatenate([buf.astype(jnp.float32), x], axis=1)
    w = w.astype(jnp.float32)
    y = w[0] * xc[:, 0:L]
    for j in range(1, CONV_W):
        y = y + w[j] * xc[:, j:j + L]
    if b is not None:
        y = y + b.astype(jnp.float32)
    return y, xc[:, L:]


def _to_chunks(t, C):
    Bn, L, H, d = t.shape
    return t.reshape(Bn, L // C, C, H, d).transpose(1, 0, 3, 2, 4)


def _from_chunks(o):
    n, Bn, H, C, e = o.shape
    return o.transpose(1, 0, 3, 2, 4).reshape(Bn, n * C, H, e)


def hgrn2_chunked(q, logf, v, S0):
    L = q.shape[1]
    C = math.gcd(L, CHUNK)
    k = -jnp.expm1(logf)
    qc, kc, vc, lc = (_to_chunks(t, C) for t in (q, k, v, logf))
    causal = jnp.tril(jnp.ones((C, C), dtype=bool))

    def step(S, inp):
        qi, ki, vi, li = inp
        b = jnp.cumsum(li, axis=-2)
        rel = b[..., :, None, :] - b[..., None, :, :]
        dec = jnp.exp(jnp.where(causal[:, :, None], rel, -jnp.inf))
        att = jnp.einsum('bhtd,bhsd,bhtsd->bhts', qi, ki, dec)
        o = (jnp.einsum('bhts,bhse->bhte', att, vi)
             + jnp.einsum('bhtd,bhde->bhte', qi * jnp.exp(b), S))
        b_last = b[..., -1:, :]
        S_new = (jnp.exp(b_last[..., 0, :])[..., None] * S
                 + jnp.einsum('bhsd,bhse->bhde', ki * jnp.exp(b_last - b), vi))
        return S_new, o

    S, o = lax.scan(step, S0, (qc, kc, vc, lc))
    return _from_chunks(o), S


def gated_delta_chunked(q, k, v, g, beta, S0):
    L = q.shape[1]
    DV = v.shape[-1]
    C = math.gcd(L, CHUNK)
    qc, kc, vc, gc, bc = (_to_chunks(t, C) for t in (q, k, v, g[..., None], beta[..., None]))
    causal = jnp.tril(jnp.ones((C, C), dtype=bool))
    strict = jnp.tril(jnp.ones((C, C), dtype=bool), -1)
    eye = jnp.eye(C, dtype=jnp.float32)

    def step(S, inp):
        qi, ki, vi, gi, bi = inp
        b = jnp.cumsum(gi[..., 0], axis=-1)
        gam = jnp.exp(jnp.where(causal, b[..., :, None] - b[..., None, :], -jnp.inf))
        A = jnp.where(strict, bi * jnp.einsum('bhtd,bhsd->bhts', ki, ki) * gam, 0.0)
        rhs = jnp.concatenate([bi * vi, bi * ki * jnp.exp(b)[..., None]], axis=-1)
        sol = lax.linalg.triangular_solve(eye + A, rhs, left_side=True, lower=True, unit_diagonal=True)
        u = sol[..., :DV] - jnp.einsum('bhtd,bhde->bhte', sol[..., DV:], S)
        qk = jnp.einsum('bhtd,bhsd->bhts', qi, ki) * gam
        o = (jnp.einsum('bhtd,bhde->bhte', qi * jnp.exp(b)[..., None], S)
             + jnp.einsum('bhts,bhse->bhte', qk, u))
        S_new = (jnp.exp(b[..., -1])[..., None, None] * S
                 + jnp.einsum('bhsd,bhse->bhde', ki * jnp.exp(b[..., -1:] - b)[..., None], u))
        return S_new, o

    S, o = lax.scan(step, S0, (qc, kc, vc, gc, bc))
    return _from_chunks(o), S


def _lin_combine(e1, e2):
    a1, b1 = e1
    a2, b2 = e2
    return a1 * a2, a2 * b1 + b2


def rglru(x, h0, w_a, b_a, w_x, b_x, lam, seq_start):
    Bn, L, _ = x.shape
    xb = x.reshape(Bn, L, C_BLOCKS, C_BLOCK)
    r = jax.nn.sigmoid(jnp.einsum('blni,nij->blnj', xb, w_a).reshape(Bn, L, C_WIDTH) + b_a)
    i = jax.nn.sigmoid(jnp.einsum('blni,nij->blnj', xb, w_x).reshape(Bn, L, C_WIDTH) + b_x)
    log_a = -RG_C * r * jax.nn.softplus(-lam.astype(jnp.float32))
    a = jnp.exp(log_a)
    mult = jnp.sqrt(-jnp.expm1(2.0 * log_a))
    if seq_start:
        mult = mult.at[:, 0].set(1.0)
    A_cum, B_cum = lax.associative_scan(_lin_combine, (a, mult * (i * x)), axis=1)
    h = B_cum + A_cum * h0[:, None, :]
    return h, h[:, -1]


def layer(x, S_a, S_b, buf_b, h_c, buf_c, lb, seq_start, prm):
    (g_mix_pre, g_mix_post, g_ffn_pre, g_ffn_post, w_in, w_out, a_norm,
     b_conv_w, b_A_log, b_dt_bias, b_norm, c_conv_w, c_conv_b, c_wa, c_ba, c_wx, c_bx, c_lam,
     w_gate, w_up, w_down) = prm
    f32 = jnp.float32
    Bn, L, _ = x.shape
    h = rmsnorm(x, g_mix_pre)
    proj = jnp.einsum('bld,de->ble', h, w_in)
    (aq, af, ai, ag, bq, bk, bv, bz, bb, ba, cx, cg) = jnp.split(proj, _split_points(), axis=-1)

    qa = jax.nn.silu(aq).reshape(Bn, L, A_HEADS, A_DK)
    lbf = lb.astype(f32)
    logf = jnp.logaddexp(jnp.log(lbf), jnp.log1p(-lbf) + jax.nn.log_sigmoid(af))
    oa, S_a_new = hgrn2_chunked(qa, logf.reshape(Bn, L, A_HEADS, A_DK),
                                ai.reshape(Bn, L, A_HEADS, A_DV), S_a.astype(f32))
    oa = (rmsnorm(oa, a_norm) * jax.nn.silu(ag.reshape(Bn, L, A_HEADS, A_DV))).reshape(Bn, L, A_WIDTH)

    qkv, buf_b_new = causal_dwconv(jnp.concatenate([bq, bk, bv], axis=-1), buf_b, b_conv_w, None)
    qb, kb, vb = jnp.split(jax.nn.silu(qkv), 3, axis=-1)
    qb = l2norm(qb.reshape(Bn, L, B_HEADS, B_DK)) * (B_DK ** -0.5)
    kb = l2norm(kb.reshape(Bn, L, B_HEADS, B_DK))
    vb = vb.reshape(Bn, L, B_HEADS, B_DV)
    beta = jax.nn.sigmoid(bb)
    gdec = -jnp.exp(b_A_log.astype(f32)) * jax.nn.softplus(ba + b_dt_bias)
    ob, S_b_new = gated_delta_chunked(qb, kb, vb, gdec, beta, S_b.astype(f32))
    ob = (rmsnorm(ob, b_norm) * jax.nn.silu(bz.reshape(Bn, L, B_HEADS, B_DV))).reshape(Bn, L, B_WIDTH)

    xcv, buf_c_new = causal_dwconv(cx, buf_c, c_conv_w, c_conv_b)
    hc, h_c_new = rglru(xcv, h_c.astype(f32), c_wa, c_ba, c_wx, c_bx, c_lam, seq_start)
    oc = hc * jax.nn.gelu(cg)

    mix = jnp.einsum('ble,ed->bld', jnp.concatenate([oa, ob, oc], axis=-1), w_out)
    x = x + rmsnorm(mix, g_mix_post).astype(x.dtype)
    h2 = rmsnorm(x, g_ffn_pre)
    ff = jnp.einsum('blf,fd->bld', jax.nn.silu(h2 @ w_gate) * (h2 @ w_up), w_down)
    x = x + rmsnorm(ff, g_ffn_post).astype(x.dtype)
    return x, (S_a_new, S_b_new, buf_b_new, h_c_new, buf_c_new)


def setup_inputs(seed: int = 0) -> dict:
    key = jax.random.key(seed)
    keys = jax.random.split(key, 40)
    cnt = [0]
    f32 = jnp.float32

    def nk():
        k = keys[cnt[0]]
        cnt[0] += 1
        return k

    def nrm(shape, s):
        return s * jax.random.normal(nk(), shape, f32)

    def gain(shape):
        return 1.0 + 0.05 * jax.random.normal(nk(), shape, f32)

    x_prompt = nrm((BATCH, SEQ, D_MODEL), 1.0)
    x_sample = nrm((DEC_BATCH, DEC_SEQ, D_MODEL), 1.0)
    state_hgrn = nrm((DEPTH, DEC_BATCH, A_HEADS, A_DK, A_DV), 0.5)
    state_gdn = nrm((DEPTH, DEC_BATCH, B_HEADS, B_DK, B_DV), 0.5)
    state_gdn_conv = nrm((DEPTH, DEC_BATCH, CONV_W - 1, 3 * B_WIDTH), 1.0)
    state_rglru_h = nrm((DEPTH, DEC_BATCH, C_WIDTH), 0.5)
    state_rglru_conv = nrm((DEPTH, DEC_BATCH, CONV_W - 1, C_WIDTH), 1.0)
    ln_mix_pre = gain((DEPTH, D_MODEL))
    ln_mix_post = gain((DEPTH, D_MODEL))
    ln_ffn_pre = gain((DEPTH, D_MODEL))
    ln_ffn_post = gain((DEPTH, D_MODEL))
    w_in = nrm((DEPTH, D_MODEL, IN_TOTAL), D_MODEL ** -0.5)
    w_out = nrm((DEPTH, MIX_WIDTH, D_MODEL), MIX_WIDTH ** -0.5)
    a_lb_logits = nrm((DEPTH, A_HEADS * A_DK), 0.5)
    a_norm = gain((DEPTH, A_DV))
    b_conv_w = nrm((DEPTH, CONV_W, 3 * B_WIDTH), CONV_W ** -0.5)
    b_A_log = jnp.log(jax.random.uniform(nk(), (DEPTH, B_HEADS), f32, 1.0, 16.0))
    dt = jnp.exp(jax.random.uniform(nk(), (DEPTH, B_HEADS), f32, math.log(1e-3), math.log(1e-1)))
    b_dt_bias = dt + jnp.log(-jnp.expm1(-dt))
    b_norm = gain((DEPTH, B_DV))
    c_conv_w = nrm((DEPTH, CONV_W, C_WIDTH), CONV_W ** -0.5)
    c_conv_b = nrm((DEPTH, C_WIDTH), 0.02)
    c_wa = nrm((DEPTH, C_BLOCKS, C_BLOCK, C_BLOCK), C_BLOCK ** -0.5)
    c_ba = nrm((DEPTH, C_WIDTH), 0.1)
    c_wx = nrm((DEPTH, C_BLOCKS, C_BLOCK, C_BLOCK), C_BLOCK ** -0.5)
    c_bx = nrm((DEPTH, C_WIDTH), 0.1)
    u = jax.random.uniform(nk(), (DEPTH, C_WIDTH), f32, 0.9, 0.999)
    a_base = u ** (1.0 / RG_C)
    c_lam = jnp.log(a_base) - jnp.log1p(-a_base)
    w_gate = nrm((DEPTH, D_MODEL, D_FF), D_MODEL ** -0.5)
    w_up = nrm((DEPTH, D_MODEL, D_FF), D_MODEL ** -0.5)
    w_down = nrm((DEPTH, D_FF, D_MODEL), D_FF ** -0.5)
    return {"x_prompt": x_prompt, "x_sample": x_sample,
            "state_hgrn": state_hgrn, "state_gdn": state_gdn, "state_gdn_conv": state_gdn_conv,
            "state_rglru_h": state_rglru_h, "state_rglru_conv": state_rglru_conv,
            "ln_mix_pre": ln_mix_pre, "ln_mix_post": ln_mix_post,
            "ln_ffn_pre": ln_ffn_pre, "ln_ffn_post": ln_ffn_post,
            "w_in": w_in, "w_out": w_out, "a_lb_logits": a_lb_logits, "a_norm": a_norm,
            "b_conv_w": b_conv_w, "b_A_log": b_A_log, "b_dt_bias": b_dt_bias, "b_norm": b_norm,
            "c_conv_w": c_conv_w, "c_conv_b": c_conv_b, "c_wa": c_wa, "c_ba": c_ba,
            "c_wx": c_wx, "c_bx": c_bx, "c_lam": c_lam,
            "w_gate": w_gate, "w_up": w_up, "w_down": w_down}


def reference(x_prompt, x_sample, state_hgrn, state_gdn, state_gdn_conv, state_rglru_h, state_rglru_conv,
              ln_mix_pre, ln_mix_post, ln_ffn_pre, ln_ffn_post, w_in, w_out, a_lb_logits, a_norm,
              b_conv_w, b_A_log, b_dt_bias, b_norm, c_conv_w, c_conv_b, c_wa, c_ba, c_wx, c_bx, c_lam,
              w_gate, w_up, w_down):
    f32 = jnp.float32
    lbs = jnp.cumsum(jax.nn.softmax(a_lb_logits.astype(f32), axis=0), axis=0)
    lbs = lbs - lbs[0:1]
    xp, xs = x_prompt, x_sample
    Bp = x_prompt.shape[0]
    new_p, new_s = [], []
    for l in range(DEPTH):
        prm = (ln_mix_pre[l], ln_mix_post[l], ln_ffn_pre[l], ln_ffn_post[l], w_in[l], w_out[l], a_norm[l],
               b_conv_w[l], b_A_log[l], b_dt_bias[l], b_norm[l], c_conv_w[l], c_conv_b[l],
               c_wa[l], c_ba[l], c_wx[l], c_bx[l], c_lam[l], w_gate[l], w_up[l], w_down[l])
        xp, sp = layer(xp,
                       jnp.zeros((Bp, A_HEADS, A_DK, A_DV), f32),
                       jnp.zeros((Bp, B_HEADS, B_DK, B_DV), f32),
                       jnp.zeros((Bp, CONV_W - 1, 3 * B_WIDTH), f32),
                       jnp.zeros((Bp, C_WIDTH), f32),
                       jnp.zeros((Bp, CONV_W - 1, C_WIDTH), f32),
                       lbs[l], True, prm)
        xs, ss = layer(xs, state_hgrn[l], state_gdn[l], state_gdn_conv[l], state_rglru_h[l],
                       state_rglru_conv[l], lbs[l], False, prm)
        new_p.append(sp)
        new_s.append(ss)

    def stk(lst, i, dt):
        return jnp.stack([s[i] for s in lst], axis=0).astype(dt)

    return (xp, xs,
            stk(new_p, 0, state_hgrn.dtype), stk(new_s, 0, state_hgrn.dtype),
            stk(new_p, 1, state_gdn.dtype), stk(new_s, 1, state_gdn.dtype),
            stk(new_p, 2, state_gdn_conv.dtype), stk(new_s, 2, state_gdn_conv.dtype),
            stk(new_p, 3, state_rglru_h.dtype), stk(new_s, 3, state_rglru_h.dtype),
            stk(new_p, 4, state_rglru_conv.dtype), stk(new_s, 4, state_rglru_conv.dtype))
```

```python
import functools

import jax
import jax.numpy as jnp
from jax import lax
from jax.experimental import pallas as pl
from jax.experimental.pallas import tpu as pltpu

F32 = jnp.float32
BF16 = jnp.bfloat16
HI = lax.Precision.HIGHEST

NORM_EPS = 1e-6
L2_EPS = 1e-6
HEAD = 128
A_HEADS = 4
B_HEADS = 8
A_WIDTH = A_HEADS * HEAD
B_WIDTH = B_HEADS * HEAD
C_WIDTH = 512
C_BLOCKS = 8
RG_C = 8.0
CONV_W = 4
SUB = 8
GATE_COLS = 128
VMEM_LIMIT = 56 * 1024 * 1024

NT = (((1,), (1,)), ((), ()))
TN = (((0,), (0,)), ((), ()))


def _silu(x):
    return x * jax.nn.sigmoid(x)


def _softplus(x):
    return jnp.maximum(x, 0.0) + jnp.log1p(jnp.exp(-jnp.abs(x)))


def _log_sigmoid(x):
    return -_softplus(-x)


def _neg_expm1(x):
    return -jnp.tanh(0.5 * x) * (jnp.exp(x) + 1.0)


def _rms(x, g):
    return x * lax.rsqrt(jnp.mean(x * x, axis=-1, keepdims=True) + NORM_EPS) * g


def _dot(a, b, dims=None):
    a = a.astype(BF16)
    b = b.astype(BF16)
    if dims is None:
        return jnp.dot(a, b, preferred_element_type=F32)
    return lax.dot_general(a, b, dims, preferred_element_type=F32)


def _dot_hi(a, b):
    return jnp.dot(a, b, precision=HI, preferred_element_type=F32)


def _row_tile(n, target):
    best = None
    for t in range(16, min(n, target) + 1, 16):
        if n % t == 0:
            best = t
    assert best is not None, n
    return best


def _params(*sem):
    return pltpu.CompilerParams(dimension_semantics=sem, vmem_limit_bytes=VMEM_LIMIT)


def _inproj_body(x_ref, g_ref, w_ref, ws_ref, o_ref, os_ref, h_scr):
    @pl.when(pl.program_id(1) == 0)
    def _():
        h = _rms(x_ref[...], g_ref[...]).astype(BF16)
        h_scr[...] = h
        os_ref[...] = jnp.dot(h, ws_ref[...], preferred_element_type=F32)

    o_ref[...] = jnp.dot(h_scr[...], w_ref[...], preferred_element_type=F32)


def _inproj(x, g, w_main, w_small):
    T, D = x.shape
    N = w_main.shape[1]
    tm = _row_tile(T, 832)
    tn = 1024
    return pl.pallas_call(
        _inproj_body,
        grid=(T // tm, N // tn),
        in_specs=[
            pl.BlockSpec((tm, D), lambda i, j: (i, 0)),
            pl.BlockSpec((1, D), lambda i, j: (0, 0)),
            pl.BlockSpec((D, tn), lambda i, j: (0, j)),
            pl.BlockSpec((D, GATE_COLS), lambda i, j: (0, 0)),
        ],
        out_specs=[
            pl.BlockSpec((tm, tn), lambda i, j: (i, j)),
            pl.BlockSpec((tm, GATE_COLS), lambda i, j: (i, 0)),
        ],
        out_shape=[jax.ShapeDtypeStruct((T, N), F32), jax.ShapeDtypeStruct((T, GATE_COLS), F32)],
        scratch_shapes=[pltpu.VMEM((tm, D), BF16)],
        compiler_params=_params("parallel", "arbitrary"),
        name="inproj",
    )(x, g, w_main, w_small)


def _outproj_body(ma_ref, mb_ref, mc_ref, x_ref, g_ref, wa_ref, wb_ref, wc_ref, o_ref):
    y = jnp.dot(ma_ref[...], wa_ref[...], preferred_element_type=F32)
    y = y + jnp.dot(mb_ref[...], wb_ref[...], preferred_element_type=F32)
    y = y + jnp.dot(mc_ref[...], wc_ref[...], preferred_element_type=F32)
    o_ref[...] = x_ref[...] + _rms(y, g_ref[...])


def _outproj(mix_a, mix_b, mix_c, x, g, w_a, w_b, w_c):
    T, D = x.shape
    tm = _row_tile(T, 416)
    row = lambda w: pl.BlockSpec((tm, w), lambda i: (i, 0))
    full = lambda a: pl.BlockSpec(a.shape, lambda i: (0, 0))
    return pl.pallas_call(
        _outproj_body,
        grid=(T // tm,),
        in_specs=[row(A_WIDTH), row(B_WIDTH), row(C_WIDTH), row(D), full(g), full(w_a), full(w_b), full(w_c)],
        out_specs=row(D),
        out_shape=jax.ShapeDtypeStruct((T, D), F32),
        compiler_params=_params("parallel"),
        name="outproj",
    )(mix_a, mix_b, mix_c, x, g, w_a, w_b, w_c)


def _ffn_body(x_ref, gpre_ref, gpost_ref, wg_ref, wu_ref, wd_ref, o_ref, h_scr):
    j = pl.program_id(1)

    @pl.when(j == 0)
    def _():
        h_scr[...] = _rms(x_ref[...], gpre_ref[...]).astype(BF16)

    h = h_scr[...]
    gate = jnp.dot(h, wg_ref[...], preferred_element_type=F32)
    up = jnp.dot(h, wu_ref[...], preferred_element_type=F32)
    part = jnp.dot((_silu(gate) * up).astype(BF16), wd_ref[...], preferred_element_type=F32)

    @pl.when(j == 0)
    def _():
        o_ref[...] = part

    @pl.when(j > 0)
    def _():
        o_ref[...] += part

    @pl.when(j == pl.num_programs(1) - 1)
    def _():
        o_ref[...] = x_ref[...] + _rms(o_ref[...], gpost_ref[...])


def _ffn(x, g_pre, g_post, w_gate, w_up, w_down):
    T, D = x.shape
    F = w_gate.shape[1]
    tm = _row_tile(T, 640)
    tf = 512
    return pl.pallas_call(
        _ffn_body,
        grid=(T // tm, F // tf),
        in_specs=[
            pl.BlockSpec((tm, D), lambda i, j: (i, 0)),
            pl.BlockSpec((1, D), lambda i, j: (0, 0)),
            pl.BlockSpec((1, D), lambda i, j: (0, 0)),
            pl.BlockSpec((D, tf), lambda i, j: (0, j)),
            pl.BlockSpec((D, tf), lambda i, j: (0, j)),
            pl.BlockSpec((tf, D), lambda i, j: (j, 0)),
        ],
        out_specs=pl.BlockSpec((tm, D), lambda i, j: (i, 0)),
        out_shape=jax.ShapeDtypeStruct((T, D), F32),
        scratch_shapes=[pltpu.VMEM((tm, D), BF16)],
        compiler_params=_params("parallel", "arbitrary"),
        name="ffn",
    )(x, g_pre, g_post, w_gate, w_up, w_down)


def _hgrn_gates(af, llb, l1m):
    bb = l1m + _log_sigmoid(af)
    return jnp.maximum(llb, bb) + jnp.log1p(jnp.exp(-jnp.abs(llb - bb)))


HGRN_BLOCK = 16
HGRN_ROWS = 128


def _hgrn_prompt_body(q_ref, f_ref, i_ref, g_ref, llb_ref, l1m_ref, an_ref, o_ref, s_ref, *, seq):
    R, NB = HGRN_ROWS, HGRN_ROWS // HGRN_BLOCK
    row = lax.broadcasted_iota(jnp.int32, (R, R), 0)
    col = lax.broadcasted_iota(jnp.int32, (R, R), 1)
    ltri = jnp.where((row >= col) & (row // HGRN_BLOCK == col // HGRN_BLOCK), 1.0, 0.0).astype(F32)
    tpos = lax.broadcasted_iota(jnp.int32, (HGRN_BLOCK, HEAD), 0)
    llb, l1m, an = llb_ref[...], l1m_ref[...], an_ref[...]

    def chunk(c, st):
        r0 = pl.multiple_of(c * R, R)
        logf = _hgrn_gates(f_ref[pl.ds(r0, R), :], llb, l1m)
        k = _neg_expm1(logf)
        q = _silu(q_ref[pl.ds(r0, R), :])
        v = i_ref[pl.ds(r0, R), :]
        bcum = _dot_hi(ltri, logf)
        outs = []
        for m in range(NB):
            sl = slice(HGRN_BLOCK * m, HGRN_BLOCK * (m + 1))
            bm, qm, km, vm = bcum[sl], q[sl], k[sl], v[sl]
            bl = bm[HGRN_BLOCK - 1:HGRN_BLOCK]
            o = _dot(qm * jnp.exp(bm), st, NT)
            for s in range(HGRN_BLOCK):
                w = jnp.exp(jnp.where(tpos >= s, bm - bm[s:s + 1], -jnp.inf))
                att = jnp.sum(qm * w * km[s:s + 1], axis=1, keepdims=True)
                o = o + att * vm[s:s + 1]
            st = jnp.exp(bl) * st + _dot(vm, km * jnp.exp(bl - bm), TN)
            outs.append(o)
        o = jnp.concatenate(outs, axis=0)
        o = _rms(o, an) * _silu(g_ref[pl.ds(r0, R), :])
        o_ref[pl.ds(r0, R), :] = o.astype(o_ref.dtype)
        return st

    st = lax.fori_loop(0, seq // R, chunk, jnp.zeros((HEAD, HEAD), F32))
    s_ref[...] = st.T


def _hgrn_prompt(proj, llb, l1m, a_norm, batch, seq):
    blk = lambda off: pl.BlockSpec((seq, HEAD), lambda b, h: (b, off + h))
    vec = pl.BlockSpec((1, HEAD), lambda b, h: (0, h))
    return pl.pallas_call(
        functools.partial(_hgrn_prompt_body, seq=seq),
        grid=(batch, A_HEADS),
        in_specs=[blk(0), blk(A_HEADS), blk(2 * A_HEADS), blk(3 * A_HEADS), vec, vec,
                  pl.BlockSpec((1, HEAD), lambda b, h: (0, 0))],
        out_specs=[pl.BlockSpec((seq, HEAD), lambda b, h: (b, h)),
                   pl.BlockSpec((None, None, HEAD, HEAD), lambda b, h: (b, h, 0, 0))],
        out_shape=[jax.ShapeDtypeStruct((batch * seq, A_WIDTH), BF16),
                   jax.ShapeDtypeStruct((batch, A_HEADS, HEAD, HEAD), F32)],
        compiler_params=_params("parallel", "parallel"),
        name="hgrn_prompt",
    )(proj, proj, proj, proj, llb, l1m, a_norm)


def _columns(rows):
    pad = jnp.zeros((HEAD - SUB * len(rows), HEAD), F32)
    return jnp.concatenate(list(rows) + [pad], axis=0).T


def _hgrn_sample_body(q_ref, f_ref, i_ref, g_ref, s_ref, llb_ref, l1m_ref, an_ref, o_ref, so_ref):
    logf = _hgrn_gates(f_ref[...], llb_ref[...], l1m_ref[...])
    fdec = jnp.exp(logf)
    k = _neg_expm1(logf)
    q = _silu(q_ref[...])
    v = i_ref[...]
    gate = _silu(g_ref[...])
    an = an_ref[...]
    outs = []
    for h in range(A_HEADS):
        sl = slice(HEAD * h, HEAD * (h + 1))
        cols = _columns([fdec[:, sl], k[:, sl], q[:, sl]])
        rows = []
        for j in range(SUB):
            s_new = cols[:, j:j + 1] * s_ref[j, h] + cols[:, SUB + j:SUB + j + 1] * v[j:j + 1, sl]
            so_ref[j, h] = s_new
            rows.append(jnp.sum(s_new * cols[:, 2 * SUB + j:2 * SUB + j + 1], axis=0, keepdims=True))
        o = jnp.concatenate(rows, axis=0)
        outs.append(_rms(o, an) * gate[:, sl])
    o_ref[...] = jnp.concatenate(outs, axis=1)


def _hgrn_sample(proj, state, layer, llb, l1m, a_norm, row0, n_tok):
    rb = row0 // SUB
    blk = lambda c: pl.BlockSpec((SUB, A_WIDTH), lambda t: (rb + t, c))
    vec = lambda a: pl.BlockSpec(a.shape, lambda t: (0, 0))
    return pl.pallas_call(
        _hgrn_sample_body,
        grid=(n_tok // SUB,),
        in_specs=[blk(0), blk(1), blk(2), blk(3),
                  pl.BlockSpec((None, SUB, A_HEADS, HEAD, HEAD), lambda t: (layer, t, 0, 0, 0)),
                  vec(llb), vec(l1m), vec(a_norm)],
        out_specs=[pl.BlockSpec((SUB, A_WIDTH), lambda t: (t, 0)),
                   pl.BlockSpec((SUB, A_HEADS, HEAD, HEAD), lambda t: (t, 0, 0, 0))],
        out_shape=[jax.ShapeDtypeStruct((n_tok, A_WIDTH), F32),
                   jax.ShapeDtypeStruct((n_tok, A_HEADS, HEAD, HEAD), F32)],
        compiler_params=_params("parallel"),
        name="hgrn_sample",
    )(proj, proj, proj, proj, state, llb, l1m, a_norm)


GDN_CHUNK = 64


def _conv_taps(prev, x, w):
    n = x.shape[0]
    ext = jnp.concatenate([prev, x], axis=0)
    base = SUB - (CONV_W - 1)
    y = w[0:1] * ext[base:base + n]
    for j in range(1, CONV_W):
        y = y + w[j:j + 1] * ext[base + j:base + j + n]
    return y, ext[n:n + SUB]


def _gdn_gates(small, alog, dtb):
    return jax.nn.sigmoid(small), -jnp.exp(alog) * _softplus(small + dtb)


def _gdn_prompt_body(q_ref, k_ref, v_ref, z_ref, sm_ref, cwq_ref, cwk_ref, cwv_ref, alog_ref, dtb_ref, bn_ref,
                     o_ref, s_ref, cq_ref, ck_ref, cv_ref, *, seq):
    C = GDN_CHUNK
    h = pl.program_id(1)
    row = lax.broadcasted_iota(jnp.int32, (C, C), 0)
    col = lax.broadcasted_iota(jnp.int32, (C, C), 1)
    ltri = jnp.where(row >= col, 1.0, 0.0).astype(F32)
    utri = jnp.where(row <= col, 1.0, 0.0).astype(F32)
    ones = jnp.ones((C, C), F32)
    lane_src = lax.broadcasted_iota(jnp.int32, (GATE_COLS, HEAD), 0)
    pick_beta = jnp.where(lane_src == h, 1.0, 0.0).astype(F32)
    pick_g = jnp.where(lane_src == B_HEADS + h, 1.0, 0.0).astype(F32)
    cwq, cwk, cwv = cwq_ref[...], cwk_ref[...], cwv_ref[...]
    alog, dtb, bn = alog_ref[...], dtb_ref[...], bn_ref[...]

    def chunk(ci, carry):
        S, pq, pk, pv = carry
        r0 = pl.multiple_of(ci * C, C)
        yq, pq = _conv_taps(pq, q_ref[pl.ds(r0, C), :], cwq)
        yk, pk = _conv_taps(pk, k_ref[pl.ds(r0, C), :], cwk)
        yv, pv = _conv_taps(pv, v_ref[pl.ds(r0, C), :], cwv)
        qc, kc, vc = _silu(yq), _silu(yk), _silu(yv)
        qn = qc * lax.rsqrt(jnp.sum(qc * qc, axis=-1, keepdims=True) + L2_EPS) * (HEAD ** -0.5)
        kn = kc * lax.rsqrt(jnp.sum(kc * kc, axis=-1, keepdims=True) + L2_EPS)
        beta_all, g_all = _gdn_gates(sm_ref[pl.ds(r0, C), :], alog, dtb)
        beta = _dot_hi(beta_all, pick_beta)
        g = _dot_hi(g_all, pick_g)
        b = _dot_hi(ltri, g)
        b_cols = _dot_hi(ones, g[:, :C] * utri)
        gam = jnp.exp(jnp.where(row >= col, b[:, :C] - b_cols, -jnp.inf))
        eb = jnp.exp(b)
        kk = _dot(kn, kn, NT)
        qk = _dot(qn, kn, NT)
        a = jnp.where(row > col, beta[:, :C] * kk * gam, 0.0)
        x = -a
        tm = x
        for _ in range(5):
            x = _dot(x, x)
            tm = tm + x + _dot(tm, x)
        rhs = jnp.concatenate([beta * vc, beta * kn * eb], axis=1)
        sol = rhs + _dot(tm, rhs)
        u = sol[:, :HEAD] - _dot(sol[:, HEAD:], S)
        o = _dot(qn * eb, S) + _dot(qk * gam, u)
        bl = b[C - 1:C]
        S = jnp.exp(bl) * S + _dot(kn * jnp.exp(bl - b), u, TN)
        o = _rms(o, bn) * _silu(z_ref[pl.ds(r0, C), :])
        o_ref[pl.ds(r0, C), :] = o.astype(o_ref.dtype)
        return S, pq, pk, pv

    zero8 = jnp.zeros((SUB, HEAD), F32)
    S, _, _, _ = lax.fori_loop(0, seq // C, chunk, (jnp.zeros((HEAD, HEAD), F32), zero8, zero8, zero8))
    s_ref[...] = S
    tail = slice(seq - (CONV_W - 1), seq)
    cq_ref[...] = q_ref[tail, :]
    ck_ref[...] = k_ref[tail, :]
    cv_ref[...] = v_ref[tail, :]


def _gdn_prompt(proj, small, conv_w, alog, dtb, b_norm, batch, seq):
    c0 = 4 * A_HEADS
    blk = lambda off: pl.BlockSpec((seq, HEAD), lambda b, h: (b, c0 + off + h))
    cw = lambda off: pl.BlockSpec((CONV_W, HEAD), lambda b, h: (0, off + h))
    vec = pl.BlockSpec((1, HEAD), lambda b, h: (0, 0))
    cst = pl.BlockSpec((None, CONV_W - 1, HEAD), lambda b, h: (b, 0, h))
    cshape = jax.ShapeDtypeStruct((batch, CONV_W - 1, B_WIDTH), F32)
    return pl.pallas_call(
        functools.partial(_gdn_prompt_body, seq=seq),
        grid=(batch, B_HEADS),
        in_specs=[blk(0), blk(B_HEADS), blk(2 * B_HEADS), blk(3 * B_HEADS),
                  pl.BlockSpec((seq, GATE_COLS), lambda b, h: (b, 0)),
                  cw(0), cw(B_HEADS), cw(2 * B_HEADS), vec, vec, vec],
        out_specs=[pl.BlockSpec((seq, HEAD), lambda b, h: (b, h)),
                   pl.BlockSpec((None, None, HEAD, HEAD), lambda b, h: (b, h, 0, 0)),
                   cst, cst, cst],
        out_shape=[jax.ShapeDtypeStruct((batch * seq, B_WIDTH), BF16),
                   jax.ShapeDtypeStruct((batch, B_HEADS, HEAD, HEAD), F32),
                   cshape, cshape, cshape],
        compiler_params=_params("parallel", "parallel"),
        name="gdn_prompt",
    )(proj, proj, proj, proj, small, conv_w, conv_w, conv_w, alog, dtb, b_norm)


def _gdn_sample_body(q_ref, k_ref, v_ref, z_ref, sm_ref, buf_ref, cw_ref, alog_ref, dtb_ref, bn_ref, s_ref,
                     o_ref, bufo_ref, so_ref):
    W = B_WIDTH
    buf = buf_ref[...]
    cw = cw_ref[...]
    xs = (q_ref[...], k_ref[...], v_ref[...])
    ys = []
    for p in range(3):
        y = cw[0:1, p * W:(p + 1) * W] * buf[:, p * W:(p + 1) * W]
        for j in range(1, CONV_W - 1):
            y = y + cw[j:j + 1, p * W:(p + 1) * W] * buf[:, (3 * j + p) * W:(3 * j + p + 1) * W]
        ys.append(_silu(y + cw[CONV_W - 1:CONV_W, p * W:(p + 1) * W] * xs[p]))
    bufo_ref[...] = jnp.concatenate([buf[:, 3 * W:], xs[0], xs[1], xs[2]], axis=1)
    qc, kc, vc = ys
    beta_all, g_all = _gdn_gates(sm_ref[...], alog_ref[...], dtb_ref[...])
    decay_all = jnp.exp(g_all)
    gate = _silu(z_ref[...])
    bn = bn_ref[...]
    outs = []
    for h in range(B_HEADS):
        sl = slice(HEAD * h, HEAD * (h + 1))
        qh, kh, vh = qc[:, sl], kc[:, sl], vc[:, sl]
        qn = qh * lax.rsqrt(jnp.sum(qh * qh, axis=-1, keepdims=True) + L2_EPS) * (HEAD ** -0.5)
        kn = kh * lax.rsqrt(jnp.sum(kh * kh, axis=-1, keepdims=True) + L2_EPS)
        cols = _columns([kn, qn])
        rows = []
        for j in range(SUB):
            kcol = cols[:, j:j + 1]
            qcol = cols[:, SUB + j:SUB + j + 1]
            beta = beta_all[j:j + 1, h:h + 1]
            decay = decay_all[j:j + 1, B_HEADS + h:B_HEADS + h + 1]
            S = s_ref[j, h]
            ks = jnp.sum(S * kcol, axis=0, keepdims=True)
            u = beta * (vh[j:j + 1] - decay * ks)
            s_new = decay * S + kcol * u
            so_ref[j, h] = s_new
            rows.append(jnp.sum(s_new * qcol, axis=0, keepdims=True))
        o = jnp.concatenate(rows, axis=0)
        outs.append(_rms(o, bn) * gate[:, sl])
    o_ref[...] = jnp.concatenate(outs, axis=1)


def _gdn_sample(proj, small, buf, state, layer, conv_w, alog, dtb, b_norm, row0, n_tok):
    rb = row0 // SUB
    W = B_WIDTH
    blk = lambda c: pl.BlockSpec((SUB, W), lambda t: (rb + t, c))
    vec = lambda a: pl.BlockSpec(a.shape, lambda t: (0, 0))
    nbuf = (CONV_W - 1) * 3 * W
    return pl.pallas_call(
        _gdn_sample_body,
        grid=(n_tok // SUB,),
        in_specs=[blk(2), blk(3), blk(4), blk(5),
                  pl.BlockSpec((SUB, GATE_COLS), lambda t: (rb + t, 0)),
                  pl.BlockSpec((None, SUB, nbuf), lambda t: (layer, t, 0)),
                  vec(conv_w), vec(alog), vec(dtb), vec(b_norm),
                  pl.BlockSpec((None, SUB, B_HEADS, HEAD, HEAD), lambda t: (layer, t, 0, 0, 0))],
        out_specs=[pl.BlockSpec((SUB, W), lambda t: (t, 0)),
                   pl.BlockSpec((SUB, nbuf), lambda t: (t, 0)),
                   pl.BlockSpec((SUB, B_HEADS, HEAD, HEAD), lambda t: (t, 0, 0, 0))],
        out_shape=[jax.ShapeDtypeStruct((n_tok, W), F32),
                   jax.ShapeDtypeStruct((n_tok, nbuf), F32),
                   jax.ShapeDtypeStruct((n_tok, B_HEADS, HEAD, HEAD), F32)],
        compiler_params=_params("parallel"),
        name="gdn_sample",
    )(proj, proj, proj, proj, small, buf, conv_w, alog, dtb, b_norm, state)


RGLRU_ROWS = 256


def _rglru_gates(y, wa, ba, wx, bx, lam):
    yb = y.astype(BF16)
    r = jax.nn.sigmoid(jnp.dot(yb, wa, preferred_element_type=F32) + ba)
    i = jax.nn.sigmoid(jnp.dot(yb, wx, preferred_element_type=F32) + bx)
    log_a = -RG_C * r * _softplus(-lam)
    return jnp.exp(log_a), jnp.sqrt(_neg_expm1(2.0 * log_a)), i * y


def _rglru_prompt_body(x_ref, g_ref, cw_ref, cb_ref, wa_ref, ba_ref, wx_ref, bx_ref, lam_ref,
                       o_ref, h_ref, c_ref, a_scr, b_scr, h_scr, *, seq):
    R = RGLRU_ROWS
    cw, cb = cw_ref[...], cb_ref[...]
    wa, ba, wx, bx, lam = wa_ref[...], ba_ref[...], wx_ref[...], bx_ref[...], lam_ref[...]
    tpos = lax.broadcasted_iota(jnp.int32, (R, C_WIDTH), 0)

    def gates(ci, prev):
        r0 = pl.multiple_of(ci * R, R)
        y, prev = _conv_taps(prev, x_ref[pl.ds(r0, R), :], cw)
        a, mult, ix = _rglru_gates(y + cb, wa, ba, wx, bx, lam)
        mult = jnp.where(tpos + r0 == 0, 1.0, mult)
        a_scr[pl.ds(r0, R), :] = a
        b_scr[pl.ds(r0, R), :] = mult * ix
        return prev

    lax.fori_loop(0, seq // R, gates, jnp.zeros((SUB, C_WIDTH), F32))

    def step(t, h):
        h = a_scr[pl.ds(t, 1), :] * h + b_scr[pl.ds(t, 1), :]
        h_scr[pl.ds(t, 1), :] = h
        return h

    h_ref[...] = lax.fori_loop(0, seq, step, jnp.zeros((1, C_WIDTH), F32), unroll=8)

    def emit(ci, c):
        r0 = pl.multiple_of(ci * R, R)
        o_ref[pl.ds(r0, R), :] = (h_scr[pl.ds(r0, R), :] * jax.nn.gelu(g_ref[pl.ds(r0, R), :])).astype(o_ref.dtype)
        return c

    lax.fori_loop(0, seq // R, emit, 0)
    c_ref[...] = x_ref[seq - (CONV_W - 1):seq, :]


def _rglru_prompt(proj, cw, cb, wa, ba, wx, bx, lam, batch, seq):
    c0 = (4 * A_WIDTH + 4 * B_WIDTH) // C_WIDTH
    full = lambda a: pl.BlockSpec(a.shape, lambda b: (0, 0))
    return pl.pallas_call(
        functools.partial(_rglru_prompt_body, seq=seq),
        grid=(batch,),
        in_specs=[pl.BlockSpec((seq, C_WIDTH), lambda b: (b, c0)),
                  pl.BlockSpec((seq, C_WIDTH), lambda b: (b, c0 + 1)),
                  full(cw), full(cb), full(wa), full(ba), full(wx), full(bx), full(lam)],
        out_specs=[pl.BlockSpec((seq, C_WIDTH), lambda b: (b, 0)),
                   pl.BlockSpec((None, 1, C_WIDTH), lambda b: (b, 0, 0)),
                   pl.BlockSpec((None, CONV_W - 1, C_WIDTH), lambda b: (b, 0, 0))],
        out_shape=[jax.ShapeDtypeStruct((batch * seq, C_WIDTH), BF16),
                   jax.ShapeDtypeStruct((batch, 1, C_WIDTH), F32),
                   jax.ShapeDtypeStruct((batch, CONV_W - 1, C_WIDTH), F32)],
        scratch_shapes=[pltpu.VMEM((seq, C_WIDTH), F32)] * 3,
        compiler_params=_params("parallel"),
        name="rglru_prompt",
    )(proj, proj, cw, cb, wa, ba, wx, bx, lam)


def _rglru_sample_body(x_ref, g_ref, buf_ref, h0_ref, cw_ref, cb_ref, wa_ref, ba_ref, wx_ref, bx_ref, lam_ref,
                       o_ref, h_ref, bufo_ref):
    W = C_WIDTH
    x, buf, cw = x_ref[...], buf_ref[...], cw_ref[...]
    y = cw[0:1] * buf[:, :W]
    for j in range(1, CONV_W - 1):
        y = y + cw[j:j + 1] * buf[:, j * W:(j + 1) * W]
    y = y + cw[CONV_W - 1:CONV_W] * x + cb_ref[...]
    a, mult, ix = _rglru_gates(y, wa_ref[...], ba_ref[...], wx_ref[...], bx_ref[...], lam_ref[...])
    h = mult * ix + a * h0_ref[...]
    h_ref[...] = h
    o_ref[...] = h * jax.nn.gelu(g_ref[...])
    bufo_ref[...] = jnp.concatenate([buf[:, W:], x], axis=1)


def _rglru_sample(proj, buf, h0, layer, cw, cb, wa, ba, wx, bx, lam, row0, n_tok):
    c0 = (4 * A_WIDTH + 4 * B_WIDTH) // C_WIDTH
    rb = row0 // n_tok
    nbuf = (CONV_W - 1) * C_WIDTH
    full = lambda a: pl.BlockSpec(a.shape, lambda t: (0, 0))
    return pl.pallas_call(
        _rglru_sample_body,
        grid=(1,),
        in_specs=[pl.BlockSpec((n_tok, C_WIDTH), lambda t: (rb, c0)),
                  pl.BlockSpec((n_tok, C_WIDTH), lambda t: (rb, c0 + 1)),
                  pl.BlockSpec((None, n_tok, nbuf), lambda t: (layer, 0, 0)),
                  pl.BlockSpec((None, n_tok, C_WIDTH), lambda t: (layer, 0, 0)),
                  full(cw), full(cb), full(wa), full(ba), full(wx), full(bx), full(lam)],
        out_specs=[pl.BlockSpec((n_tok, C_WIDTH), lambda t: (0, 0)),
                   pl.BlockSpec((n_tok, C_WIDTH), lambda t: (0, 0)),
                   pl.BlockSpec((n_tok, nbuf), lambda t: (0, 0))],
        out_shape=[jax.ShapeDtypeStruct((n_tok, C_WIDTH), F32),
                   jax.ShapeDtypeStruct((n_tok, C_WIDTH), F32),
                   jax.ShapeDtypeStruct((n_tok, nbuf), F32)],
        compiler_params=_params("arbitrary"),
        name="rglru_sample",
    )(proj, proj, buf, h0, cw, cb, wa, ba, wx, bx, lam)


def _block_diag(w):
    n, a, b = w.shape
    eye = jnp.eye(n, dtype=w.dtype)
    return (eye[:, None, :, None] * w[:, :, None, :]).reshape(n * a, n * b)


def _lane_row(v, offset):
    return jnp.zeros((1, GATE_COLS), F32).at[0, offset:offset + v.shape[0]].set(v.astype(F32))


def kernel(x_prompt, x_sample, state_hgrn, state_gdn, state_gdn_conv, state_rglru_h, state_rglru_conv,
           ln_mix_pre, ln_mix_post, ln_ffn_pre, ln_ffn_post, w_in, w_out, a_lb_logits, a_norm,
           b_conv_w, b_A_log, b_dt_bias, b_norm, c_conv_w, c_conv_b, c_wa, c_ba, c_wx, c_bx, c_lam,
           w_gate, w_up, w_down):
    depth = w_in.shape[0]
    batch, seq, d_model = x_prompt.shape
    n_tok = x_sample.shape[0]
    t_prompt = batch * seq
    gate0 = 4 * A_WIDTH + 4 * B_WIDTH
    row = lambda v: v.astype(F32).reshape(1, -1)

    lbs = jnp.cumsum(jax.nn.softmax(a_lb_logits.astype(F32), axis=0), axis=0)
    lbs = lbs - lbs[0:1]
    log_lb, log_1m_lb = jnp.log(lbs), jnp.log1p(-lbs)

    gdn_buf = state_gdn_conv.reshape(depth, n_tok, -1)
    rglru_buf = state_rglru_conv.reshape(depth, n_tok, -1)

    x = jnp.concatenate([x_prompt.reshape(t_prompt, d_model), x_sample.reshape(n_tok, d_model)], axis=0)
    new_p, new_s = [], []
    for l in range(depth):
        w_main = jnp.concatenate([w_in[l][:, :gate0], w_in[l][:, gate0 + 2 * B_HEADS:]], axis=1).astype(BF16)
        w_small = jnp.pad(w_in[l][:, gate0:gate0 + 2 * B_HEADS], ((0, 0), (0, GATE_COLS - 2 * B_HEADS))).astype(BF16)
        proj, small = _inproj(x, row(ln_mix_pre[l]), w_main, w_small)

        llb, l1m, an = row(log_lb[l]), row(log_1m_lb[l]), row(a_norm[l])
        oa_p, sa_p = _hgrn_prompt(proj, llb, l1m, an, batch, seq)
        oa_s, sa_s = _hgrn_sample(proj, state_hgrn, l, llb, l1m, an, t_prompt, n_tok)

        alog, dtb, bn = _lane_row(b_A_log[l], B_HEADS), _lane_row(b_dt_bias[l], B_HEADS), row(b_norm[l])
        bcw = b_conv_w[l].astype(F32)
        ob_p, sb_p, cq, ck, cv = _gdn_prompt(proj, small, bcw, alog, dtb, bn, batch, seq)
        ob_s, bufb_s, sb_s = _gdn_sample(proj, small, gdn_buf, state_gdn, l, bcw, alog, dtb, bn, t_prompt, n_tok)

        ccw, ccb = c_conv_w[l].astype(F32), row(c_conv_b[l])
        wa, wx = _block_diag(c_wa[l]).astype(BF16), _block_diag(c_wx[l]).astype(BF16)
        cba, cbx, lam = row(c_ba[l]), row(c_bx[l]), row(c_lam[l])
        oc_p, hc_p, cc_p = _rglru_prompt(proj, ccw, ccb, wa, cba, wx, cbx, lam, batch, seq)
        oc_s, hc_s, bufc_s = _rglru_sample(proj, rglru_buf, state_rglru_h, l, ccw, ccb, wa, cba, wx, cbx, lam,
                                           t_prompt, n_tok)

        mix_a = jnp.concatenate([oa_p, oa_s.astype(BF16)], axis=0)
        mix_b = jnp.concatenate([ob_p, ob_s.astype(BF16)], axis=0)
        mix_c = jnp.concatenate([oc_p, oc_s.astype(BF16)], axis=0)
        wo = w_out[l]
        x = _outproj(mix_a, mix_b, mix_c, x, row(ln_mix_post[l]),
                     wo[:A_WIDTH].astype(BF16), wo[A_WIDTH:A_WIDTH + B_WIDTH].astype(BF16),
                     wo[A_WIDTH + B_WIDTH:].astype(BF16))
        x = _ffn(x, row(ln_ffn_pre[l]), row(ln_ffn_post[l]),
                 w_gate[l].astype(BF16), w_up[l].astype(BF16), w_down[l].astype(BF16))

        new_p.append((sa_p, sb_p, jnp.concatenate([cq, ck, cv], axis=-1), hc_p.reshape(batch, C_WIDTH), cc_p))
        new_s.append((sa_s, sb_s, bufb_s.reshape(n_tok, CONV_W - 1, 3 * B_WIDTH), hc_s,
                      bufc_s.reshape(n_tok, CONV_W - 1, C_WIDTH)))

    stk = lambda lst, i, ref: jnp.stack([s[i] for s in lst], axis=0).astype(ref.dtype)
    return (x[:t_prompt].reshape(batch, seq, d_model), x[t_prompt:].reshape(n_tok, 1, d_model),
            stk(new_p, 0, state_hgrn), stk(new_s, 0, state_hgrn),
            stk(new_p, 1, state_gdn), stk(new_s, 1, state_gdn),
            stk(new_p, 2, state_gdn_conv), stk(new_s, 2, state_gdn_conv),
            stk(new_p, 3, state_rglru_h), stk(new_s, 3, state_rglru_h),
            stk(new_p, 4, state_rglru_conv), stk(new_s, 4, state_rglru_conv))
```
